```python
import jax, jax.numpy as jnp
from jax import lax
import numpy as np

D_MODEL = 1024
BATCH = 16
SEQ = 2048
DEPTH = 2

CHUNK = 64
HEAD_DIM = 64
N_HEADS = D_MODEL // HEAD_DIM
N_HEADS_A = N_HEADS // 2
N_HEADS_B = N_HEADS - N_HEADS_A
WIDTH_A = N_HEADS_A * HEAD_DIM
WIDTH_B = N_HEADS_B * HEAD_DIM
IN_COLS = 3 * WIDTH_A + 4 * WIDTH_B
LEFT_CHUNKS = 8
BAND = (LEFT_CHUNKS + 1) * CHUNK
MAX_REL = 128
ROPE_BASE = 10000.0
CONV_WIDTH = 31
PEER_HEADS = 8
PEER_NKEYS = 128
PEER_EXPERTS = PEER_NKEYS * PEER_NKEYS
PEER_QDIM = 128
PEER_TOPK = 16
PEER_BLOCK = 128
N_EVEN = (DEPTH + 1) // 2
N_ODD = DEPTH // 2
EPS = 1e-6

kernel_name = "hybrid_chunkattn_retention_conformerconv_peer"


def rms_norm(x, g):
    x32 = x.astype(jnp.float32)
    y = x32 * lax.rsqrt(jnp.mean(x32 * x32, axis=-1, keepdims=True) + EPS)
    return (y * g.astype(jnp.float32)).astype(x.dtype)


def layer_norm(x, g, b):
    x32 = x.astype(jnp.float32)
    mu = jnp.mean(x32, axis=-1, keepdims=True)
    var = jnp.mean(jnp.square(x32 - mu), axis=-1, keepdims=True)
    y = (x32 - mu) * lax.rsqrt(var + EPS)
    return (y * g.astype(jnp.float32) + b.astype(jnp.float32)).astype(x.dtype)


def rope(t, pos):
    half = HEAD_DIM // 2
    inv = ROPE_BASE ** (-jnp.arange(half, dtype=jnp.float32) / half)
    ang = pos.astype(jnp.float32)[:, None] * inv[None, :]
    cos = jnp.cos(ang)[None, :, None, :]
    sin = jnp.sin(ang)[None, :, None, :]
    t1 = t[..., :half].astype(jnp.float32)
    t2 = t[..., half:].astype(jnp.float32)
    return jnp.concatenate([t1 * cos - t2 * sin, t1 * sin + t2 * cos], axis=-1).astype(t.dtype)


def chunk_attention(q, k, v, rel_table):
    b, s, h, dh = q.shape
    nc = s // CHUNK
    qi = jnp.arange(CHUNK)[:, None]
    kj = jnp.arange(BAND)[None, :]
    rel = jnp.clip(LEFT_CHUNKS * CHUNK + qi - kj, -MAX_REL, MAX_REL) + MAX_REL
    bias = rel_table[:, rel].astype(jnp.float32)
    pad = jnp.zeros((b, LEFT_CHUNKS * CHUNK, h, dh), k.dtype)
    k_pad = jnp.concatenate([pad, k], axis=1)
    v_pad = jnp.concatenate([pad, v], axis=1)
    scale = dh ** -0.5

    def one_chunk(c):
        start = c * CHUNK
        q_c = lax.dynamic_slice_in_dim(q, start, CHUNK, axis=1)
        k_c = lax.dynamic_slice_in_dim(k_pad, start, BAND, axis=1)
        v_c = lax.dynamic_slice_in_dim(v_pad, start, BAND, axis=1)
        sc = jnp.einsum('bihd,bjhd->bhij', q_c, k_c).astype(jnp.float32) * scale + bias[None]
        valid = jnp.arange(BAND) >= (LEFT_CHUNKS - c) * CHUNK
        sc = jnp.where(valid[None, None, None, :], sc, -jnp.inf)
        p = jax.nn.softmax(sc, axis=-1).astype(v.dtype)
        return jnp.einsum('bhij,bjhd->bihd', p, v_c)

    out = lax.map(one_chunk, jnp.arange(nc))
    return out.transpose(1, 0, 2, 3, 4).reshape(b, s, h, dh)


def retention(q, k, v, g):
    b, s, h, dh = q.shape
    nc = s // CHUNK
    pos = jnp.arange(s)
    q = rope(q, pos)
    k = (rope(k, pos).astype(jnp.float32) * dh ** -0.5).astype(q.dtype)
    gamma = 1.0 - 2.0 ** (-5.0 - jnp.arange(h, dtype=jnp.float32))
    log_g = jnp.log(gamma)
    idx = jnp.arange(CHUNK, dtype=jnp.float32)
    intra_decay = jnp.exp(log_g[:, None, None] * jnp.abs(idx[:, None] - idx[None, :])).astype(q.dtype)
    kv_decay = jnp.exp(log_g[None, :] * (CHUNK - 1 - idx)[:, None]).astype(q.dtype)
    q_decay = jnp.exp(log_g[None, :] * (idx + 1.0)[:, None]).astype(q.dtype)
    chunk_decay = jnp.exp(log_g * CHUNK).astype(q.dtype)

    qc = q.reshape(b, nc, CHUNK, h, dh)
    kc = k.reshape(b, nc, CHUNK, h, dh)
    vc = v.reshape(b, nc, CHUNK, h, dh)

    scores = jnp.einsum('bnihd,bnjhd->bnhij', qc, kc) * intra_decay[None, None]
    intra = jnp.einsum('bnhij,bnjhe->bnihe', scores, vc)

    kv = jnp.einsum('bnjhd,bnjhe->nbhde', kc * kv_decay[None, None, :, :, None], vc)

    def step(state, kv_n):
        return chunk_decay[None, :, None, None] * state + kv_n, state

    _, prev = lax.scan(step, jnp.zeros_like(kv[0]), kv)
    cross = jnp.einsum('bnihd,nbhde->bnihe', qc, prev) * q_decay[None, None, :, :, None]

    o32 = (intra + cross).astype(jnp.float32)
    mu = jnp.mean(o32, axis=-1, keepdims=True)
    var = jnp.mean(jnp.square(o32 - mu), axis=-1, keepdims=True)
    o = ((o32 - mu) * lax.rsqrt(var + EPS)).reshape(b, s, h, dh)
    return (o * jax.nn.silu(g.astype(jnp.float32))).astype(q.dtype)


def hybrid_mixer(xn, w_in, w_out, rel_table):
    b, s, _ = xn.shape
    proj = xn @ w_in
    cuts = [WIDTH_A, 2 * WIDTH_A, 3 * WIDTH_A, 3 * WIDTH_A + WIDTH_B,
            3 * WIDTH_A + 2 * WIDTH_B, 3 * WIDTH_A + 3 * WIDTH_B]
    qa, ka, va, qb, kb, vb, gb = jnp.split(proj, cuts, axis=-1)
    heads_a = lambda t: t.reshape(b, s, N_HEADS_A, HEAD_DIM)
    heads_b = lambda t: t.reshape(b, s, N_HEADS_B, HEAD_DIM)
    oa = chunk_attention(heads_a(qa), heads_a(ka), heads_a(va), rel_table)
    ob = retention(heads_b(qb), heads_b(kb), heads_b(vb), heads_b(gb))
    o = jnp.concatenate([oa.reshape(b, s, WIDTH_A), ob.reshape(b, s, WIDTH_B)], axis=-1)
    return o @ w_out


def conv_module(xn, w1, b1, w_dw, b_dw, ln_g, ln_b, w2, b2):
    hid = xn @ w1 + b1
    a, gate = jnp.split(hid, 2, axis=-1)
    hid = a * jax.nn.sigmoid(gate)
    hid = lax.conv_general_dilated(
        hid, w_dw[:, None, :], window_strides=(1,), padding=[(CONV_WIDTH - 1, 0)],
        dimension_numbers=('NWC', 'WIO', 'NWC'), feature_group_count=D_MODEL) + b_dw
    hid = jax.nn.silu(layer_norm(hid, ln_g, ln_b))
    return hid @ w2 + b2


def peer(xn, w_q, sub_keys, u, v):
    b, s, d = xn.shape
    t = b * s
    xf = xn.reshape(t, d)
    q = (xf @ w_q).reshape(t, PEER_HEADS, PEER_QDIM)
    half = PEER_QDIM // 2
    s1 = jnp.einsum('thd,kd->thk', q[..., :half], sub_keys[0]).astype(jnp.float32)
    s2 = jnp.einsum('thd,kd->thk', q[..., half:], sub_keys[1]).astype(jnp.float32)
    top1, i1 = lax.top_k(s1, PEER_TOPK)
    top2, i2 = lax.top_k(s2, PEER_TOPK)
    cand = (top1[..., :, None] + top2[..., None, :]).reshape(t, PEER_HEADS, PEER_TOPK * PEER_TOPK)
    cidx = (i1[..., :, None] * PEER_NKEYS + i2[..., None, :]).reshape(t, PEER_HEADS, PEER_TOPK * PEER_TOPK)
    best, pos = lax.top_k(cand, PEER_TOPK)
    eidx = jnp.take_along_axis(cidx, pos, axis=-1)
    gates = jax.nn.softmax(best, axis=-1).astype(xn.dtype)
    nsel = PEER_HEADS * PEER_TOPK
    nb = t // PEER_BLOCK
    xb = xf.reshape(nb, PEER_BLOCK, d)
    ib = eidx.reshape(nb, PEER_BLOCK, nsel)
    gbk = gates.reshape(nb, PEER_BLOCK, nsel)

    def expert_block(args):
        x_blk, i_blk, g_blk = args
        u_sel = u[i_blk]
        act = jax.nn.gelu(jnp.einsum('pkd,pd->pk', u_sel, x_blk), approximate=False) * g_blk
        return jnp.einsum('pk,pkd->pd', act, v[i_blk])

    out = lax.map(expert_block, (xb, ib, gbk))
    return out.reshape(b, s, d)


def setup_inputs(seed: int = 0) -> dict:
    key = jax.random.key(seed)
    ks = jax.random.split(key, 24)
    D = D_MODEL

    def nrm(k, shape, scale):
        return jax.random.normal(k, shape, jnp.float32) * scale

    return {
        "x": nrm(ks[0], (BATCH, SEQ, D), 1.0),
        "mix_norm": 1.0 + nrm(ks[1], (DEPTH, D), 0.01),
        "ffn_norm": 1.0 + nrm(ks[2], (DEPTH, D), 0.01),
        "final_norm": 1.0 + nrm(ks[3], (D,), 0.01),
        "w_in": nrm(ks[4], (N_EVEN, D, IN_COLS), D ** -0.5),
        "w_out": nrm(ks[5], (N_EVEN, D, D), D ** -0.5),
        "rel_bias": nrm(ks[6], (N_EVEN, N_HEADS_A, 2 * MAX_REL + 1), 0.1),
        "conv_w1": nrm(ks[7], (N_ODD, D, 2 * D), D ** -0.5),
        "conv_b1": nrm(ks[8], (N_ODD, 2 * D), 0.01),
        "conv_dw": nrm(ks[9], (N_ODD, CONV_WIDTH, D), CONV_WIDTH ** -0.5),
        "conv_dw_b": nrm(ks[10], (N_ODD, D), 0.01),
        "conv_ln_g": 1.0 + nrm(ks[11], (N_ODD, D), 0.01),
        "conv_ln_b": nrm(ks[12], (N_ODD, D), 0.01),
        "conv_w2": nrm(ks[13], (N_ODD, D, D), D ** -0.5),
        "conv_b2": nrm(ks[14], (N_ODD, D), 0.01),
        "peer_wq": nrm(ks[15], (DEPTH, D, PEER_HEADS * PEER_QDIM), D ** -0.5),
        "peer_keys": nrm(ks[16], (DEPTH, 2, PEER_NKEYS, PEER_QDIM // 2), (PEER_QDIM // 2) ** -0.5),
        "peer_u": nrm(ks[17], (DEPTH, PEER_EXPERTS, D), D ** -0.5),
        "peer_v": nrm(ks[18], (DEPTH, PEER_EXPERTS, D), PEER_HEADS ** -0.5),
    }


def reference(x, mix_norm, ffn_norm, final_norm, w_in, w_out, rel_bias,
              conv_w1, conv_b1, conv_dw, conv_dw_b, conv_ln_g, conv_ln_b, conv_w2, conv_b2,
              peer_wq, peer_keys, peer_u, peer_v):
    h = x
    for layer in range(DEPTH):
        xn = rms_norm(h, mix_norm[layer])
        if layer % 2 == 0:
            e = layer // 2
            h = h + hybrid_mixer(xn, w_in[e], w_out[e], rel_bias[e])
        else:
            o = layer // 2
            h = h + conv_module(xn, conv_w1[o], conv_b1[o], conv_dw[o], conv_dw_b[o],
                                conv_ln_g[o], conv_ln_b[o], conv_w2[o], conv_b2[o])
        xn = rms_norm(h, ffn_norm[layer])
        h = h + peer(xn, peer_wq[layer], peer_keys[layer], peer_u[layer], peer_v[layer])
    return rms_norm(h, final_norm)
```

```python
import functools

import jax
import jax.numpy as jnp
from jax import lax
from jax.experimental import pallas as pl
from jax.experimental.pallas import tpu as pltpu

F32 = jnp.float32
BF16 = jnp.bfloat16

CHUNK = 64
HEAD_DIM = 64
LEFT_CHUNKS = 8
BAND = (LEFT_CHUNKS + 1) * CHUNK
MAX_REL = 128
ROPE_BASE = 10000.0
CONV_WIDTH = 31
PEER_HEADS = 8
PEER_NKEYS = 128
PEER_TOPK = 16
EPS = 1e-6

LANES = 128
PAIR = LANES // HEAD_DIM
VMEM_LIMIT = 56 * 1024 * 1024

_PAIRS = [(i, j) for i in range(PEER_TOPK) for j in range(PEER_TOPK) if (i + 1) * (j + 1) <= PEER_TOPK]


def _cparams(*sem):
    return pltpu.CompilerParams(dimension_semantics=sem, vmem_limit_bytes=VMEM_LIMIT)


def _rms(x, g):
    return x * lax.rsqrt(jnp.mean(x * x, axis=-1, keepdims=True) + EPS) * g


def _norm_proj_kernel(h_ref, g_ref, w_ref, o_ref, *, col_chunk):
    xn = _rms(h_ref[...], g_ref[...]).astype(BF16)
    for n0 in range(0, w_ref.shape[1], col_chunk):
        o_ref[:, n0:n0 + col_chunk] = jnp.dot(
            xn, w_ref[:, n0:n0 + col_chunk], preferred_element_type=F32).astype(o_ref.dtype)


def _norm_proj(h, g, w, *, tm=512, col_chunk=512):
    t, d = h.shape
    n = w.shape[1]
    return pl.pallas_call(
        functools.partial(_norm_proj_kernel, col_chunk=col_chunk),
        grid=(t // tm,),
        in_specs=[pl.BlockSpec((tm, d), lambda i: (i, 0)),
                  pl.BlockSpec((1, d), lambda i: (0, 0)),
                  pl.BlockSpec((d, n), lambda i: (0, 0))],
        out_specs=pl.BlockSpec((tm, n), lambda i: (i, 0)),
        out_shape=jax.ShapeDtypeStruct((t, n), BF16),
        compiler_params=_cparams("parallel"),
        name="norm_proj",
    )(h, g, w)


def _attn_kernel(q_ref, k_ref, v_ref, bias_ref, o_ref, kpad, vpad):
    s = q_ref.shape[1]
    nc = s // CHUNK
    pad = LEFT_CHUNKS * CHUNK
    kpad[0:pad, :] = jnp.zeros((pad, LANES), BF16)
    vpad[0:pad, :] = jnp.zeros((pad, LANES), BF16)
    kpad[pad:pad + s, :] = k_ref[0]
    vpad[pad:pad + s, :] = v_ref[0]
    lane = lax.broadcasted_iota(jnp.int32, (CHUNK, LANES), 1)
    head_mask = [lane < HEAD_DIM, lane >= HEAD_DIM]
    col = lax.broadcasted_iota(jnp.int32, (CHUNK, BAND), 1)
    scale = HEAD_DIM ** -0.5

    def body(c, carry):
        r0 = pl.multiple_of(c * CHUNK, CHUNK)
        qc = q_ref[0, pl.ds(r0, CHUNK), :]
        kb = kpad[pl.ds(r0, BAND), :]
        vb = vpad[pl.ds(r0, BAND), :]
        valid = col >= (LEFT_CHUNKS - c) * CHUNK
        outs = []
        for hh in range(PAIR):
            qm = jnp.where(head_mask[hh], qc, jnp.zeros_like(qc))
            sc = lax.dot_general(qm, kb, (((1,), (1,)), ((), ())), preferred_element_type=F32)
            sc = sc * scale + bias_ref[hh]
            sc = jnp.where(valid, sc, -jnp.inf)
            m = jnp.max(sc, axis=-1, keepdims=True)
            e = jnp.exp(sc - m)
            l = jnp.sum(e, axis=-1, keepdims=True)
            outs.append(jnp.dot(e.astype(BF16), vb, preferred_element_type=F32) / l)
        o = jnp.where(head_mask[0], outs[0], outs[1])
        o_ref[0, pl.ds(r0, CHUNK), :] = o.astype(o_ref.dtype)
        return carry

    lax.fori_loop(0, nc, body, 0)


def _chunk_attention(proj, bias, *, n_pairs):
    b, s, _ = proj.shape
    blk = lambda off: pl.BlockSpec((1, s, LANES), lambda i, p, off=off: (i, 0, off + p))
    return pl.pallas_call(
        _attn_kernel,
        grid=(b, n_pairs),
        in_specs=[blk(0), blk(n_pairs), blk(2 * n_pairs),
                  pl.BlockSpec((PAIR, CHUNK, BAND), lambda i, p: (p, 0, 0))],
        out_specs=pl.BlockSpec((1, s, LANES), lambda i, p: (i, 0, p)),
        out_shape=jax.ShapeDtypeStruct((b, s, n_pairs * LANES), BF16),
        scratch_shapes=[pltpu.VMEM((s + LEFT_CHUNKS * CHUNK, LANES), BF16),
                        pltpu.VMEM((s + LEFT_CHUNKS * CHUNK, LANES), BF16)],
        compiler_params=_cparams("parallel", "parallel"),
        name="chunk_attention",
    )(proj, proj, proj, bias)


def _retention_kernel(q_ref, k_ref, v_ref, g_ref, cos_ref, sin_ref, idec_ref, qdec_ref, kvdec_ref,
                      cdec_ref, o_ref, state):
    s = q_ref.shape[1]
    nc = s // CHUNK
    state[...] = jnp.zeros_like(state)
    lane = lax.broadcasted_iota(jnp.int32, (CHUNK, LANES), 1)
    head_mask = [lane < HEAD_DIM, lane >= HEAD_DIM]
    low_half = (lane % HEAD_DIM) < (HEAD_DIM // 2)
    row_b = lax.broadcasted_iota(jnp.int32, (LANES, LANES), 0) < HEAD_DIM
    col_b = lax.broadcasted_iota(jnp.int32, (LANES, LANES), 1) < HEAD_DIM
    same_head = row_b == col_b
    inv_dh = 1.0 / HEAD_DIM

    def rope(t, cos, sin_signed):
        partner = jnp.where(low_half, pltpu.roll(t, LANES - HEAD_DIM // 2, 1), pltpu.roll(t, HEAD_DIM // 2, 1))
        return t * cos + partner * sin_signed

    def head_sum(x):
        s0 = jnp.sum(jnp.where(head_mask[0], x, 0.0), axis=-1, keepdims=True)
        s1 = jnp.sum(jnp.where(head_mask[1], x, 0.0), axis=-1, keepdims=True)
        return jnp.where(head_mask[0], s0, s1)

    def body(n, carry):
        r0 = pl.multiple_of(n * CHUNK, CHUNK)
        rows = pl.ds(r0, CHUNK)
        cos = cos_ref[rows, :]
        sin = sin_ref[rows, :]
        qr = rope(q_ref[0, rows, :].astype(F32), cos, sin)
        kr = rope(k_ref[0, rows, :].astype(F32), cos, sin) * (HEAD_DIM ** -0.5)
        v = v_ref[0, rows, :]
        g = g_ref[0, rows, :].astype(F32)
        qb = qr.astype(BF16)
        kb = kr.astype(BF16)
        outs = []
        for hh in range(PAIR):
            qm = jnp.where(head_mask[hh], qb, jnp.zeros_like(qb))
            sc = lax.dot_general(qm, kb, (((1,), (1,)), ((), ())), preferred_element_type=F32)
            sc = sc * idec_ref[hh]
            outs.append(jnp.dot(sc.astype(BF16), v, preferred_element_type=F32))
        intra = jnp.where(head_mask[0], outs[0], outs[1])
        st = state[...]
        cross = jnp.dot(qb, st.astype(BF16), preferred_element_type=F32) * qdec_ref[0]
        o = intra + cross
        kd_t = (kr * kvdec_ref[0]).T.astype(BF16)
        kv = jnp.dot(kd_t, v, preferred_element_type=F32)
        state[...] = cdec_ref[0] * st + jnp.where(same_head, kv, 0.0)
        mu = head_sum(o) * inv_dh
        dlt = o - mu
        var = head_sum(dlt * dlt) * inv_dh
        y = dlt * lax.rsqrt(var + EPS)
        o_ref[0, rows, :] = (y * (g / (1.0 + jnp.exp(-g)))).astype(o_ref.dtype)
        return carry

    lax.fori_loop(0, nc, body, 0)


def _retention(proj, tables, *, col0, n_pairs):
    b, s, _ = proj.shape
    cos, sin, idec, qdec, kvdec, cdec = tables
    blk = lambda off: pl.BlockSpec((1, s, LANES), lambda i, p, off=off: (i, 0, col0 + off + p))
    return pl.pallas_call(
        _retention_kernel,
        grid=(b, n_pairs),
        in_specs=[blk(0), blk(n_pairs), blk(2 * n_pairs), blk(3 * n_pairs),
                  pl.BlockSpec((s, LANES), lambda i, p: (0, 0)),
                  pl.BlockSpec((s, LANES), lambda i, p: (0, 0)),
                  pl.BlockSpec((PAIR, CHUNK, CHUNK), lambda i, p: (p, 0, 0)),
                  pl.BlockSpec((1, CHUNK, LANES), lambda i, p: (p, 0, 0)),
                  pl.BlockSpec((1, CHUNK, LANES), lambda i, p: (p, 0, 0)),
                  pl.BlockSpec((1, 1, LANES), lambda i, p: (p, 0, 0))],
        out_specs=pl.BlockSpec((1, s, LANES), lambda i, p: (i, 0, p)),
        out_shape=jax.ShapeDtypeStruct((b, s, n_pairs * LANES), BF16),
        scratch_shapes=[pltpu.VMEM((LANES, LANES), F32)],
        compiler_params=_cparams("parallel", "parallel"),
        name="retention",
    )(proj, proj, proj, proj, cos, sin, idec, qdec, kvdec, cdec)


def _retention_tables(s, n_heads):
    half = HEAD_DIM // 2
    inv = ROPE_BASE ** (-jnp.arange(half, dtype=F32) / half)
    ang = jnp.arange(s, dtype=F32)[:, None] * inv[None, :]
    cos = jnp.tile(jnp.cos(ang), (1, LANES // half))
    sgn = jnp.tile(jnp.concatenate([-jnp.ones((half,), F32), jnp.ones((half,), F32)]), LANES // HEAD_DIM)
    sin = jnp.tile(jnp.sin(ang), (1, LANES // half)) * sgn[None, :]
    gamma = 1.0 - 2.0 ** (-5.0 - jnp.arange(n_heads, dtype=F32))
    log_g = jnp.log(gamma)
    idx = jnp.arange(CHUNK, dtype=F32)
    idec = jnp.exp(log_g[:, None, None] * jnp.abs(idx[:, None] - idx[None, :]))
    per_lane = lambda t: jnp.repeat(t, HEAD_DIM, axis=-1).reshape(t.shape[0], n_heads // PAIR, LANES).transpose(1, 0, 2)
    kvdec = per_lane(jnp.exp(log_g[None, :] * (CHUNK - 1 - idx)[:, None]))
    qdec = per_lane(jnp.exp(log_g[None, :] * (idx + 1.0)[:, None]))
    cdec = per_lane(jnp.exp(log_g * CHUNK)[None, :])
    return cos, sin, idec, qdec, kvdec, cdec


def _out_proj_kernel(h_ref, oa_ref, ob_ref, w_ref, o_ref):
    wa = oa_ref.shape[1]
    acc = jnp.dot(oa_ref[...], w_ref[0:wa, :], preferred_element_type=F32)
    acc += jnp.dot(ob_ref[...], w_ref[wa:, :], preferred_element_type=F32)
    o_ref[...] = h_ref[...] + acc


def _out_proj(h, oa, ob, w, *, tm=512):
    t, d = h.shape
    return pl.pallas_call(
        _out_proj_kernel,
        grid=(t // tm,),
        in_specs=[pl.BlockSpec((tm, d), lambda i: (i, 0)),
                  pl.BlockSpec((tm, oa.shape[1]), lambda i: (i, 0)),
                  pl.BlockSpec((tm, ob.shape[1]), lambda i: (i, 0)),
                  pl.BlockSpec(w.shape, lambda i: (0, 0))],
        out_specs=pl.BlockSpec((tm, d), lambda i: (i, 0)),
        out_shape=jax.ShapeDtypeStruct((t, d), F32),
        compiler_params=_cparams("parallel"),
        name="out_proj",
    )(h, oa, ob, w)


def _router_kernel(h_ref, g_ref, wq_ref, kk_ref, xt_ref, s1_ref, s2_ref, e1_ref, e2_ref, tau_ref,
                   t1s, t2s):
    xn = _rms(h_ref[...], g_ref[...])
    xt_ref[...] = xn.T.astype(BF16)
    q = jnp.dot(xn.astype(BF16), wq_ref[...], preferred_element_type=F32)
    for h in range(PEER_HEADS):
        qh = q[:, h * LANES:(h + 1) * LANES].astype(BF16)
        st = lax.dot_general(kk_ref[...], qh, (((1,), (1,)), ((), ())), preferred_element_type=F32)
        s1 = st[:PEER_NKEYS]
        s2 = st[PEER_NKEYS:]
        s1_ref[h] = s1
        s2_ref[h] = s2
        for sc, ts in ((s1, t1s), (s2, t2s)):
            m = jnp.max(sc, axis=0, keepdims=True)
            ts[0, h:h + 1, :] = m
            for k in range(1, PEER_TOPK):
                m = jnp.max(jnp.where(sc < m, sc, -jnp.inf), axis=0, keepdims=True)
                ts[k, h:h + 1, :] = m
    cands = [t1s[i] + t2s[j] for (i, j) in _PAIRS]
    top = cands[0]
    m = top
    for _ in range(1, PEER_TOPK):
        m = functools.reduce(jnp.maximum, [jnp.where(c < m, c, -jnp.inf) for c in cands])
    tau = m
    z = functools.reduce(jnp.add, [jnp.where(c >= tau, jnp.exp(c - top), 0.0) for c in cands])
    inv_z = 1.0 / z
    tau_ref[...] = tau
    for h in range(PEER_HEADS):
        e1_ref[h] = jnp.exp(s1_ref[h] - t1s[0, h:h + 1, :])
        e2_ref[h] = jnp.exp(s2_ref[h] - t2s[0, h:h + 1, :]) * inv_z[h:h + 1, :]


def _router(h, g, wq, kk, *, tm=256):
    t, d = h.shape
    nh, nk = PEER_HEADS, PEER_NKEYS
    tok3 = pl.BlockSpec((nh, nk, tm), lambda i: (0, 0, i))
    sd3 = jax.ShapeDtypeStruct((nh, nk, t), F32)
    return pl.pallas_call(
        _router_kernel,
        grid=(t // tm,),
        in_specs=[pl.BlockSpec((tm, d), lambda i: (i, 0)),
                  pl.BlockSpec((1, d), lambda i: (0, 0)),
                  pl.BlockSpec(wq.shape, lambda i: (0, 0)),
                  pl.BlockSpec(kk.shape, lambda i: (0, 0))],
        out_specs=[pl.BlockSpec((d, tm), lambda i: (0, i)), tok3, tok3, tok3, tok3,
                   pl.BlockSpec((nh, tm), lambda i: (0, i))],
        out_shape=[jax.ShapeDtypeStruct((d, t), BF16), sd3, sd3, sd3, sd3,
                   jax.ShapeDtypeStruct((nh, t), F32)],
        scratch_shapes=[pltpu.VMEM((PEER_TOPK, nh, tm), F32), pltpu.VMEM((PEER_TOPK, nh, tm), F32)],
        compiler_params=_cparams("parallel"),
        name="peer_router",
    )(h, g, wq, kk)


SLAB_ROWS = 16
SLAB_COLS = 256


def _experts_kernel(xt_ref, s1_ref, s2_ref, e1_ref, e2_ref, tau_ref, h_ref, gf_ref, u_ref, vt_ref, o_ref,
                    acc, act, prob, s1b, e1b, taub, *, final_norm):
    j = pl.program_id(1)
    te, tm = act.shape
    na = te // PEER_NKEYS

    @pl.when(j == 0)
    def _():
        acc[...] = jnp.zeros_like(acc)
        for h in range(PEER_HEADS):
            taub[h] = jnp.broadcast_to(tau_ref[h:h + 1, :], (SLAB_ROWS, tm))

    for h in range(PEER_HEADS):
        for a in range(na):
            row = j * na + a
            s1b[h, a] = jnp.broadcast_to(s1_ref[h, pl.ds(row, 1), :], (SLAB_ROWS, tm))
            e1b[h, a] = jnp.broadcast_to(e1_ref[h, pl.ds(row, 1), :], (SLAB_ROWS, tm))

    act[...] = jnp.dot(u_ref[...], xt_ref[...], preferred_element_type=F32)

    for a in range(na):
        for rg in range(PEER_NKEYS // SLAB_ROWS):
            for cg in range(tm // SLAB_COLS):
                brow = slice(rg * SLAB_ROWS, (rg + 1) * SLAB_ROWS)
                erow = slice(a * PEER_NKEYS + rg * SLAB_ROWS, a * PEER_NKEYS + (rg + 1) * SLAB_ROWS)
                cols = slice(cg * SLAB_COLS, (cg + 1) * SLAB_COLS)
                gate = jnp.zeros((SLAB_ROWS, SLAB_COLS), F32)
                for h in range(PEER_HEADS):
                    pair_sum = s2_ref[h, brow, cols] + s1b[h, a, :, cols]
                    gate += jnp.where(pair_sum >= taub[h, :, cols],
                                      e2_ref[h, brow, cols] * e1b[h, a, :, cols], 0.0)
                x = act[erow, cols]
                gelu = 0.5 * x * (1.0 + lax.erf(x * (2.0 ** -0.5)))
                prob[erow, cols] = (gelu * gate).astype(BF16)

    acc[...] += jnp.dot(vt_ref[...], prob[...], preferred_element_type=F32)

    @pl.when(j == pl.num_programs(1) - 1)
    def _():
        out = h_ref[...] + acc[...].T
        if final_norm:
            out = _rms(out, gf_ref[...])
        o_ref[...] = out


def _experts(xt, s1, s2, e1, e2, tau, h, gf, u, vt, *, final_norm, tm=512, te=512):
    t, d = h.shape
    ne = u.shape[0]
    nh, nk = PEER_HEADS, PEER_NKEYS
    tok3 = pl.BlockSpec((nh, nk, tm), lambda i, j: (0, 0, i))
    return pl.pallas_call(
        functools.partial(_experts_kernel, final_norm=final_norm),
        grid=(t // tm, ne // te),
        in_specs=[pl.BlockSpec((d, tm), lambda i, j: (0, i)), tok3, tok3, tok3, tok3,
                  pl.BlockSpec((nh, tm), lambda i, j: (0, i)),
                  pl.BlockSpec((tm, d), lambda i, j: (i, 0)),
                  pl.BlockSpec((1, d), lambda i, j: (0, 0)),
                  pl.BlockSpec((te, d), lambda i, j: (j, 0)),
                  pl.BlockSpec((d, te), lambda i, j: (0, j))],
        out_specs=pl.BlockSpec((tm, d), lambda i, j: (i, 0)),
        out_shape=jax.ShapeDtypeStruct((t, d), F32),
        scratch_shapes=[pltpu.VMEM((d, tm), F32),
                        pltpu.VMEM((te, tm), F32),
                        pltpu.VMEM((te, tm), BF16),
                        pltpu.VMEM((nh, te // nk, SLAB_ROWS, tm), F32),
                        pltpu.VMEM((nh, te // nk, SLAB_ROWS, tm), F32),
                        pltpu.VMEM((nh, SLAB_ROWS, tm), F32)],
        compiler_params=_cparams("parallel", "arbitrary"),
        name="peer_experts",
    )(xt, s1, s2, e1, e2, tau, h, gf, u, vt)


def _peer(h, g, gf, wq, keys, u, v, *, final_norm):
    d_half = keys.shape[-1]
    zeros = jnp.zeros((PEER_NKEYS, d_half), F32)
    kk = jnp.concatenate([jnp.concatenate([keys[0], zeros], axis=1),
                          jnp.concatenate([zeros, keys[1]], axis=1)], axis=0).astype(BF16)
    xt, s1, s2, e1, e2, tau = _router(h, g, wq.astype(BF16), kk)
    return _experts(xt, s1, s2, e1, e2, tau, h, gf, u.astype(BF16), v.T.astype(BF16), final_norm=final_norm)


def _glu_kernel(h_ref, g_ref, w_ref, b_ref, o_ref):
    d = o_ref.shape[1]
    xn = _rms(h_ref[...], g_ref[...]).astype(BF16)
    a = jnp.dot(xn, w_ref[:, 0:d], preferred_element_type=F32) + b_ref[:, 0:d]
    gate = jnp.dot(xn, w_ref[:, d:], preferred_element_type=F32) + b_ref[:, d:]
    o_ref[...] = a / (1.0 + jnp.exp(-gate))


def _glu(h, g, w, b, *, tm=512):
    t, d = h.shape
    return pl.pallas_call(
        _glu_kernel,
        grid=(t // tm,),
        in_specs=[pl.BlockSpec((tm, d), lambda i: (i, 0)),
                  pl.BlockSpec((1, d), lambda i: (0, 0)),
                  pl.BlockSpec(w.shape, lambda i: (0, 0)),
                  pl.BlockSpec(b.shape, lambda i: (0, 0))],
        out_specs=pl.BlockSpec((tm, d), lambda i: (i, 0)),
        out_shape=jax.ShapeDtypeStruct((t, d), F32),
        compiler_params=_cparams("parallel"),
        name="conv_glu",
    )(h, g, w, b)


HALO = 32


def _dwconv_kernel(cur_ref, prev_ref, h_ref, wdw_ref, bdw_ref, lng_ref, lnb_ref, w2_ref, b2_ref, o_ref, ext):
    ts, d = cur_ref.shape[1], cur_ref.shape[2]
    first_tile = pl.program_id(1) == 0
    ext[0:HALO, :] = jnp.where(first_tile, 0.0, prev_ref[0])
    ext[HALO:HALO + ts, :] = cur_ref[0]
    off = HALO - (CONV_WIDTH - 1)
    acc = jnp.zeros((ts, d), F32) + bdw_ref[...]
    for k in range(CONV_WIDTH):
        acc += ext[off + k:off + k + ts, :] * wdw_ref[k:k + 1, :]
    mu = jnp.mean(acc, axis=-1, keepdims=True)
    dlt = acc - mu
    var = jnp.mean(dlt * dlt, axis=-1, keepdims=True)
    y = dlt * lax.rsqrt(var + EPS) * lng_ref[...] + lnb_ref[...]
    y = y / (1.0 + jnp.exp(-y))
    o_ref[0] = h_ref[0] + jnp.dot(y.astype(BF16), w2_ref[...], preferred_element_type=F32) + b2_ref[...]


def _dwconv(hid, h, wdw, bdw, lng, lnb, w2, b2, *, ts=256):
    b, s, d = hid.shape
    per_tile = ts // HALO
    row = pl.BlockSpec((1, d), lambda i, n: (0, 0))
    tile = pl.BlockSpec((1, ts, d), lambda i, n: (i, n, 0))
    return pl.pallas_call(
        _dwconv_kernel,
        grid=(b, s // ts),
        in_specs=[tile,
                  pl.BlockSpec((1, HALO, d), lambda i, n: (i, jnp.maximum(n * per_tile - 1, 0), 0)),
                  tile,
                  pl.BlockSpec(wdw.shape, lambda i, n: (0, 0)), row, row, row,
                  pl.BlockSpec(w2.shape, lambda i, n: (0, 0)), row],
        out_specs=tile,
        out_shape=jax.ShapeDtypeStruct((b, s, d), F32),
        scratch_shapes=[pltpu.VMEM((HALO + ts, d), F32)],
        compiler_params=_cparams("parallel", "parallel"),
        name="conv_dw",
    )(hid, hid, h, wdw, bdw, lng, lnb, w2, b2)


def _rel_bias_table(rel_table):
    qi = jnp.arange(CHUNK)[:, None]
    kj = jnp.arange(BAND)[None, :]
    rel = jnp.clip(LEFT_CHUNKS * CHUNK + qi - kj, -MAX_REL, MAX_REL) + MAX_REL
    return rel_table[:, rel].astype(F32)


def kernel(x, mix_norm, ffn_norm, final_norm, w_in, w_out, rel_bias, conv_w1, conv_b1, conv_dw, conv_dw_b,
           conv_ln_g, conv_ln_b, conv_w2, conv_b2, peer_wq, peer_keys, peer_u, peer_v):
    b, s, d = x.shape
    t = b * s
    depth = mix_norm.shape[0]
    n_heads_a = rel_bias.shape[1]
    width_a = n_heads_a * HEAD_DIM
    n_heads_b = (w_in.shape[2] - 3 * width_a) // (4 * HEAD_DIM)
    pairs_a = n_heads_a // PAIR
    pairs_b = n_heads_b // PAIR
    row = lambda vec: vec.reshape(1, -1)
    gf = row(final_norm)

    h = x.reshape(t, d)
    for layer in range(depth):
        g_mix = row(mix_norm[layer])
        if layer % 2 == 0:
            e = layer // 2
            proj = _norm_proj(h, g_mix, w_in[e].astype(BF16)).reshape(b, s, -1)
            oa = _chunk_attention(proj, _rel_bias_table(rel_bias[e]), n_pairs=pairs_a)
            ob = _retention(proj, _retention_tables(s, n_heads_b), col0=3 * pairs_a, n_pairs=pairs_b)
            h = _out_proj(h, oa.reshape(t, -1), ob.reshape(t, -1), w_out[e].astype(BF16))
        else:
            o = layer // 2
            hid = _glu(h, g_mix, conv_w1[o].astype(BF16), row(conv_b1[o]))
            h = _dwconv(hid.reshape(b, s, d), h.reshape(b, s, d), conv_dw[o], row(conv_dw_b[o]),
                        row(conv_ln_g[o]), row(conv_ln_b[o]), conv_w2[o].astype(BF16),
                        row(conv_b2[o])).reshape(t, d)
        h = _peer(h, row(ffn_norm[layer]), gf, peer_wq[layer], peer_keys[layer], peer_u[layer], peer_v[layer],
                  final_norm=(layer == depth - 1))
    return h.reshape(b, s, d)
```
